```python
import jax, jax.numpy as jnp
from jax import lax
import numpy as np

D_MODEL = 1024
BATCH = 4
SEQ = 4096
DEPTH = 2
DEC_BATCH = 128
DEC_SEQ = 8
PAST_LEN = 2048
PAGE_SIZE = 128

N_MIXERS = 2
N_CONV_LAYERS = (DEPTH + 1) // 2
N_ATTN_LAYERS = DEPTH // 2
D_CONV = D_MODEL
CONV_WIDTH = 31
N_HEADS = 16
HEAD_DIM = D_MODEL // N_HEADS
Q_BLOCK = 128
SB_SCALE = HEAD_DIM ** -0.5
SB_BIAS_INIT = -6.0
N_GROUPS = 4
EXPERTS_PER_GROUP = 4
N_EXPERTS = N_GROUPS * EXPERTS_PER_GROUP
TOP_K = 2
D_EXPERT = 512
D_PLE = 256
EPS = 1e-6

kernel_name = 'conv_stickbreak_hmoe_decoder_step'


def rmsnorm(x, g):
    xf = x.astype(jnp.float32)
    y = xf * lax.rsqrt(jnp.mean(xf * xf, axis=-1, keepdims=True) + EPS)
    return (y * g.astype(jnp.float32)).astype(x.dtype)


def layernorm(x, g, b):
    xf = x.astype(jnp.float32)
    mu = jnp.mean(xf, axis=-1, keepdims=True)
    xc = xf - mu
    y = xc * lax.rsqrt(jnp.mean(xc * xc, axis=-1, keepdims=True) + EPS)
    return (y * g.astype(jnp.float32) + b.astype(jnp.float32)).astype(x.dtype)


def glu_in(u, w1, b1):
    a = u @ w1 + b1
    return a[..., :D_CONV] * jax.nn.sigmoid(a[..., D_CONV:])


def depthwise_causal(xp, w_dw):
    return lax.conv_general_dilated(
        xp, w_dw[:, None, :].astype(xp.dtype), window_strides=(1,), padding='VALID',
        dimension_numbers=('NWC', 'WIO', 'NWC'), feature_group_count=xp.shape[-1])


def conv_out(c, b_dw, g_ln, b_ln, w2, b2):
    c = layernorm(c + b_dw, g_ln, b_ln)
    return jax.nn.silu(c) @ w2 + b2


def conv_mixer_prompt(u, w1, b1, w_dw, b_dw, g_ln, b_ln, w2, b2):
    g = glu_in(u, w1, b1)
    gp = jnp.pad(g, ((0, 0), (CONV_WIDTH - 1, 0), (0, 0)))
    y = conv_out(depthwise_causal(gp, w_dw), b_dw, g_ln, b_ln, w2, b2)
    return y, gp[:, -(CONV_WIDTH - 1):]


def conv_mixer_sample(u, state, w1, b1, w_dw, b_dw, g_ln, b_ln, w2, b2):
    g = glu_in(u, w1, b1)
    gc = jnp.concatenate([state.astype(g.dtype), g], axis=1)
    y = conv_out(depthwise_causal(gc, w_dw), b_dw, g_ln, b_ln, w2, b2)
    return y, gc[:, -(CONV_WIDTH - 1):]


def qkv_proj(u, w_qkv, g_q, g_k):
    b, t, _ = u.shape
    qkv = (u @ w_qkv).reshape(b, t, 3, N_HEADS, HEAD_DIM)
    return rmsnorm(qkv[:, :, 0], g_q), rmsnorm(qkv[:, :, 1], g_k), qkv[:, :, 2]


def stick_breaking(q, k, v, bias, q_pos, k_pos):
    z = jnp.einsum('bqhd,bkhd->bhqk', q, k, preferred_element_type=jnp.float32) * SB_SCALE
    z = z + bias.astype(jnp.float32)[None, :, None, None]
    mask = k_pos[None, :] < q_pos[:, None]
    log_beta = jax.nn.log_sigmoid(z)
    log_keep = jnp.where(mask, jax.nn.log_sigmoid(-z), 0.0)
    after = lax.cumsum(log_keep, axis=3, reverse=True) - log_keep
    a = jnp.where(mask, jnp.exp(log_beta + after), 0.0)
    return jnp.einsum('bhqk,bkhd->bqhd', a.astype(v.dtype), v)


def sb_attn_prompt(u, w_qkv, g_q, g_k, bias, w_o):
    b, t, _ = u.shape
    q, k, v = qkv_proj(u, w_qkv, g_q, g_k)
    outs = []
    for start in range(0, t, Q_BLOCK):
        end = min(start + Q_BLOCK, t)
        outs.append(stick_breaking(q[:, start:end], k[:, :end], v[:, :end], bias,
                                   jnp.arange(start, end), jnp.arange(end)))
    o = jnp.concatenate(outs, axis=1).reshape(b, t, D_MODEL)
    return o @ w_o, k, v


def sb_attn_sample(u, cache_k, cache_v, layer, page_table, w_qkv, g_q, g_k, bias, w_o):
    b, t, _ = u.shape
    q, k, v = qkv_proj(u, w_qkv, g_q, g_k)
    past = page_table.shape[1] * cache_k.shape[2]
    pk = cache_k[layer][page_table].reshape(b, past, N_HEADS, HEAD_DIM).astype(k.dtype)
    pv = cache_v[layer][page_table].reshape(b, past, N_HEADS, HEAD_DIM).astype(v.dtype)
    k_all = jnp.concatenate([pk, k], axis=1)
    v_all = jnp.concatenate([pv, v], axis=1)
    o = stick_breaking(q, k_all, v_all, bias, past + jnp.arange(t), jnp.arange(past + t))
    return o.reshape(b, t, D_MODEL) @ w_o, k, v


def hier_moe(u, w_rg, w_re, w_g, w_u, w_d):
    gl = (u @ w_rg).astype(jnp.float32)
    gp = jax.nn.softmax(gl, axis=-1)
    gi = jnp.argmax(gl, axis=-1)
    g_prob = jnp.max(gp, axis=-1, keepdims=True)
    el = (u @ w_re).astype(jnp.float32).reshape(u.shape[:-1] + (N_GROUPS, EXPERTS_PER_GROUP))
    el_sel = jnp.sum(el * jax.nn.one_hot(gi, N_GROUPS, dtype=jnp.float32)[..., None], axis=-2)
    tv, ti = lax.top_k(el_sel, TOP_K)
    tw = jax.nn.softmax(tv, axis=-1) * g_prob
    flat = gi[..., None] * EXPERTS_PER_GROUP + ti
    combine = jnp.sum(jax.nn.one_hot(flat, N_EXPERTS, dtype=jnp.float32) * tw[..., None], axis=-2)
    hg = jnp.einsum('btd,edf->btef', u, w_g)
    hu = jnp.einsum('btd,edf->btef', u, w_u)
    act = jax.nn.silu(hg) * hu * combine[..., None].astype(u.dtype)
    return jnp.einsum('btef,efd->btd', act, w_d)


def ple_add(h, p, g, w_p, w_pg):
    gate = jax.nn.sigmoid(rmsnorm(h, g) @ w_pg)
    return h + (p @ w_p) * gate


def setup_inputs(seed: int = 0) -> dict:
    key = jax.random.key(seed)
    ks = iter(jax.random.split(key, 40))
    f32 = jnp.float32

    def nrm(shape, scale):
        return jax.random.normal(next(ks), shape, f32) * scale

    n_pages = PAST_LEN // PAGE_SIZE
    n_phys = (DEC_BATCH * n_pages * 5) // 4
    perm = jax.random.permutation(next(ks), n_phys)
    page_table = perm[:DEC_BATCH * n_pages].reshape(DEC_BATCH, n_pages).astype(jnp.int32)
    return {
        'x_prompt': nrm((BATCH, SEQ, D_MODEL), 1.0),
        'x_sample': nrm((DEC_BATCH, DEC_SEQ, D_MODEL), 1.0),
        'state_conv': nrm((N_CONV_LAYERS, DEC_BATCH, CONV_WIDTH - 1, D_CONV), 1.0),
        'cache_k': nrm((N_ATTN_LAYERS, n_phys, PAGE_SIZE, N_HEADS, HEAD_DIM), 1.0),
        'cache_v': nrm((N_ATTN_LAYERS, n_phys, PAGE_SIZE, N_HEADS, HEAD_DIM), 1.0),
        'page_table': page_table,
        'p_prompt': nrm((DEPTH, BATCH, SEQ, D_PLE), 1.0),
        'p_sample': nrm((DEPTH, DEC_BATCH, DEC_SEQ, D_PLE), 1.0),
        'g_mix': 1.0 + nrm((DEPTH, D_MODEL), 0.05),
        'g_ffn': 1.0 + nrm((DEPTH, D_MODEL), 0.05),
        'g_ple': 1.0 + nrm((DEPTH, D_MODEL), 0.05),
        'conv_w1': nrm((N_CONV_LAYERS, D_MODEL, 2 * D_CONV), D_MODEL ** -0.5),
        'conv_b1': nrm((N_CONV_LAYERS, 2 * D_CONV), 0.02),
        'conv_wdw': nrm((N_CONV_LAYERS, CONV_WIDTH, D_CONV), CONV_WIDTH ** -0.5),
        'conv_bdw': nrm((N_CONV_LAYERS, D_CONV), 0.02),
        'conv_gln': 1.0 + nrm((N_CONV_LAYERS, D_CONV), 0.05),
        'conv_bln': nrm((N_CONV_LAYERS, D_CONV), 0.02),
        'conv_w2': nrm((N_CONV_LAYERS, D_CONV, D_MODEL), D_CONV ** -0.5),
        'conv_b2': nrm((N_CONV_LAYERS, D_MODEL), 0.02),
        'attn_wqkv': nrm((N_ATTN_LAYERS, D_MODEL, 3 * D_MODEL), D_MODEL ** -0.5),
        'attn_gq': 1.0 + nrm((N_ATTN_LAYERS, HEAD_DIM), 0.05),
        'attn_gk': 1.0 + nrm((N_ATTN_LAYERS, HEAD_DIM), 0.05),
        'attn_bias': SB_BIAS_INIT + nrm((N_ATTN_LAYERS, N_HEADS), 0.1),
        'attn_wo': nrm((N_ATTN_LAYERS, D_MODEL, D_MODEL), D_MODEL ** -0.5),
        'moe_wrg': nrm((DEPTH, D_MODEL, N_GROUPS), D_MODEL ** -0.5),
        'moe_wre': nrm((DEPTH, D_MODEL, N_EXPERTS), D_MODEL ** -0.5),
        'moe_wg': nrm((DEPTH, N_EXPERTS, D_MODEL, D_EXPERT), D_MODEL ** -0.5),
        'moe_wu': nrm((DEPTH, N_EXPERTS, D_MODEL, D_EXPERT), D_MODEL ** -0.5),
        'moe_wd': nrm((DEPTH, N_EXPERTS, D_EXPERT, D_MODEL), D_EXPERT ** -0.5),
        'ple_w': nrm((DEPTH, D_PLE, D_MODEL), D_PLE ** -0.5),
        'ple_wg': nrm((DEPTH, D_MODEL, D_MODEL), D_MODEL ** -0.5),
    }


def reference(x_prompt, x_sample, state_conv, cache_k, cache_v, page_table, p_prompt, p_sample,
              g_mix, g_ffn, g_ple, conv_w1, conv_b1, conv_wdw, conv_bdw, conv_gln, conv_bln,
              conv_w2, conv_b2, attn_wqkv, attn_gq, attn_gk, attn_bias, attn_wo,
              moe_wrg, moe_wre, moe_wg, moe_wu, moe_wd, ple_w, ple_wg):
    hp, hs = x_prompt, x_sample
    conv_p, conv_s, k_p, v_p, k_s, v_s = [], [], [], [], [], []
    for i in range(DEPTH):
        up = rmsnorm(hp, g_mix[i])
        us = rmsnorm(hs, g_mix[i])
        if i % N_MIXERS == 0:
            c = i // N_MIXERS
            cw = (conv_w1[c], conv_b1[c], conv_wdw[c], conv_bdw[c], conv_gln[c], conv_bln[c],
                  conv_w2[c], conv_b2[c])
            mp, sp = conv_mixer_prompt(up, *cw)
            ms, ss = conv_mixer_sample(us, state_conv[c], *cw)
            conv_p.append(sp)
            conv_s.append(ss)
        else:
            a = i // N_MIXERS
            aw = (attn_wqkv[a], attn_gq[a], attn_gk[a], attn_bias[a], attn_wo[a])
            mp, kp_, vp_ = sb_attn_prompt(up, *aw)
            ms, ks_, vs_ = sb_attn_sample(us, cache_k, cache_v, a, page_table, *aw)
            k_p.append(kp_)
            v_p.append(vp_)
            k_s.append(ks_)
            v_s.append(vs_)
        hp = hp + mp
        hs = hs + ms
        mw = (moe_wrg[i], moe_wre[i], moe_wg[i], moe_wu[i], moe_wd[i])
        hp = hp + hier_moe(rmsnorm(hp, g_ffn[i]), *mw)
        hs = hs + hier_moe(rmsnorm(hs, g_ffn[i]), *mw)
        hp = ple_add(hp, p_prompt[i], g_ple[i], ple_w[i], ple_wg[i])
        hs = ple_add(hs, p_sample[i], g_ple[i], ple_w[i], ple_wg[i])
    return (hp, hs, jnp.stack(conv_p), jnp.stack(conv_s), jnp.stack(k_p), jnp.stack(v_p),
            jnp.stack(k_s), jnp.stack(v_s))
```

```python
import functools

import jax
import jax.numpy as jnp
from jax import lax
from jax.experimental import pallas as pl
from jax.experimental.pallas import tpu as pltpu

F32 = jnp.float32
BF16 = jnp.bfloat16

D_MODEL = 1024
N_HEADS = 16
HEAD_DIM = 64
CONV_WIDTH = 31
N_GROUPS = 4
EXPERTS_PER_GROUP_LOG2 = 2
N_EXPERTS = 16
D_EXPERT = 512
D_PLE = 256
EPS = 1e-6
SB_SCALE = HEAD_DIM ** -0.5

LANES = 128
VMEM_LIMIT_BYTES = 56 * 1024 * 1024

CONV_HALO = 32
ATTN_BLOCK = 256
ROUTER_LANES = 128


def _cparams(*sem):
    return pltpu.CompilerParams(dimension_semantics=sem, vmem_limit_bytes=VMEM_LIMIT_BYTES)


def _sigmoid(x):
    return 1.0 / (1.0 + jnp.exp(-x))


def _rms(x, g):
    return x * lax.rsqrt(jnp.mean(x * x, axis=-1, keepdims=True) + EPS) * g


def _split_bf16(x):
    hi = x.astype(BF16)
    lo = (x - hi.astype(F32)).astype(BF16)
    return hi, lo


def _dot(a, b):
    return jnp.dot(a, b, preferred_element_type=F32)


def _dot_nt(a, b):
    return lax.dot_general(a, b, (((1,), (1,)), ((), ())), preferred_element_type=F32)


def _glu_kernel(h_ref, g_ref, w1_ref, b1_ref, o_ref):
    u = _rms(h_ref[...], g_ref[...]).astype(BF16)
    a = _dot(u, w1_ref[...]) + b1_ref[...]
    o_ref[...] = a[:, :D_MODEL] * _sigmoid(a[:, D_MODEL:])


def _glu(h, g, w1, b1, tm):
    n = h.shape[0]
    return pl.pallas_call(
        _glu_kernel,
        grid=(n // tm,),
        in_specs=[
            pl.BlockSpec((tm, D_MODEL), lambda i: (i, 0)),
            pl.BlockSpec((1, D_MODEL), lambda i: (0, 0)),
            pl.BlockSpec((D_MODEL, 2 * D_MODEL), lambda i: (0, 0)),
            pl.BlockSpec((1, 2 * D_MODEL), lambda i: (0, 0)),
        ],
        out_specs=pl.BlockSpec((tm, D_MODEL), lambda i: (i, 0)),
        out_shape=jax.ShapeDtypeStruct((n, D_MODEL), F32),
        compiler_params=_cparams("parallel"),
        name="glu",
    )(h, g, w1, b1)


def _conv_tail(c, h, bdw_ref, gln_ref, bln_ref, w2_ref, b2_ref):
    c = c + bdw_ref[...]
    xc = c - jnp.mean(c, axis=-1, keepdims=True)
    y = xc * lax.rsqrt(jnp.mean(xc * xc, axis=-1, keepdims=True) + EPS) * gln_ref[...] + bln_ref[...]
    s = (y * _sigmoid(y)).astype(BF16)
    return h + _dot(s, w2_ref[...]) + b2_ref[...]


def _conv_prompt_kernel(g_ref, halo_ref, h_ref, wdw_ref, bdw_ref, gln_ref, bln_ref, w2_ref, b2_ref,
                        o_ref, xg_ref, c_ref, *, rows_chunk, lanes_chunk):
    tq = g_ref.shape[0]
    xg_ref[:CONV_HALO, :] = jnp.where(pl.program_id(1) == 0, 0.0, halo_ref[...])
    xg_ref[CONV_HALO:, :] = g_ref[...]
    first = CONV_HALO - (CONV_WIDTH - 1)

    def chunk(c, carry):
        r0 = pl.multiple_of(c * rows_chunk, rows_chunk)
        for l0 in range(0, D_MODEL, lanes_chunk):
            xs = xg_ref[pl.ds(r0, rows_chunk + CONV_HALO), l0:l0 + lanes_chunk]
            w = wdw_ref[:, l0:l0 + lanes_chunk]
            acc = jnp.zeros((rows_chunk, lanes_chunk), F32)
            for k in range(CONV_WIDTH):
                acc = acc + xs[first + k:first + k + rows_chunk, :] * w[k:k + 1, :]
            c_ref[pl.ds(r0, rows_chunk), l0:l0 + lanes_chunk] = acc
        return carry

    lax.fori_loop(0, tq // rows_chunk, chunk, 0)
    o_ref[...] = _conv_tail(c_ref[...], h_ref[...], bdw_ref, gln_ref, bln_ref, w2_ref, b2_ref)


def _conv_prompt(g3, h3, cw, tq):
    nb, t, _ = g3.shape
    blocks = tq // CONV_HALO
    kern = functools.partial(_conv_prompt_kernel, rows_chunk=32, lanes_chunk=256)
    row = lambda b, i: (0, 0)
    tile = pl.BlockSpec((None, tq, D_MODEL), lambda b, i: (b, i, 0))
    return pl.pallas_call(
        kern,
        grid=(nb, t // tq),
        in_specs=[
            tile,
            pl.BlockSpec((None, CONV_HALO, D_MODEL),
                         lambda b, i: (b, jnp.maximum(i * blocks - 1, 0), 0)),
            tile,
            pl.BlockSpec((CONV_WIDTH, D_MODEL), row),
            pl.BlockSpec((1, D_MODEL), row),
            pl.BlockSpec((1, D_MODEL), row),
            pl.BlockSpec((1, D_MODEL), row),
            pl.BlockSpec((D_MODEL, D_MODEL), row),
            pl.BlockSpec((1, D_MODEL), row),
        ],
        out_specs=tile,
        out_shape=jax.ShapeDtypeStruct((nb, t, D_MODEL), F32),
        scratch_shapes=[
            pltpu.VMEM((tq + CONV_HALO, D_MODEL), F32),
            pltpu.VMEM((tq, D_MODEL), F32),
        ],
        compiler_params=_cparams("parallel", "arbitrary"),
        name="conv_prompt",
    )(g3, g3, h3, *cw)


def _conv_sample_kernel(st_ref, g_ref, h_ref, wdw_ref, bdw_ref, gln_ref, bln_ref, w2_ref, b2_ref,
                        o_ref, nst_ref):
    n_state = st_ref.shape[0]
    t, bb, _ = g_ref.shape

    def row(j):
        return st_ref[j] if j < n_state else g_ref[j - n_state]

    accs = []
    for i in range(t):
        acc = jnp.zeros((bb, D_MODEL), F32)
        for k in range(CONV_WIDTH):
            acc = acc + row(i + k) * wdw_ref[k:k + 1, :]
        accs.append(acc)
    c = jnp.concatenate(accs, axis=0)
    out = _conv_tail(c, h_ref[...].reshape(t * bb, D_MODEL), bdw_ref, gln_ref, bln_ref, w2_ref, b2_ref)
    o_ref[...] = out.reshape(t, bb, D_MODEL)
    for j in range(n_state):
        nst_ref[j] = row(t + j)


def _conv_sample(state, g3, h3, cw, bb):
    n_state, nb, _ = state.shape
    t = g3.shape[0]
    row = lambda b: (0, 0)
    tok = pl.BlockSpec((t, bb, D_MODEL), lambda b: (0, b, 0))
    st = pl.BlockSpec((n_state, bb, D_MODEL), lambda b: (0, b, 0))
    return pl.pallas_call(
        _conv_sample_kernel,
        grid=(nb // bb,),
        in_specs=[
            st, tok, tok,
            pl.BlockSpec((CONV_WIDTH, D_MODEL), row),
            pl.BlockSpec((1, D_MODEL), row),
            pl.BlockSpec((1, D_MODEL), row),
            pl.BlockSpec((1, D_MODEL), row),
            pl.BlockSpec((D_MODEL, D_MODEL), row),
            pl.BlockSpec((1, D_MODEL), row),
        ],
        out_specs=[tok, st],
        out_shape=[
            jax.ShapeDtypeStruct((t, nb, D_MODEL), F32),
            jax.ShapeDtypeStruct((n_state, nb, D_MODEL), F32),
        ],
        compiler_params=_cparams("parallel"),
        name="conv_sample",
    )(state, g3, h3, *cw)


def _head_norm_rows(x, g, gmat, gmat_t):
    hi, lo = _split_bf16(x * x)
    ms = _dot(hi, gmat) + _dot(lo, gmat)
    rhi, rlo = _split_bf16(lax.rsqrt(ms + EPS))
    return x * (_dot(rhi, gmat_t) + _dot(rlo, gmat_t)) * g


def _head_norm_cols(xt, g_col):
    tm = xt.shape[1]
    x3 = xt.reshape(N_HEADS, HEAD_DIM, tm)
    ms = jnp.mean(x3 * x3, axis=1, keepdims=True)
    return (x3 * lax.rsqrt(ms + EPS)).reshape(D_MODEL, tm) * g_col


def _qkv_kernel(*refs, emit_rows, emit_bf16):
    (h_ref, g_ref, wq_ref, wkt_ref, wvt_ref, gq_ref, gkc_ref, gmat_ref, gmat_t_ref) = refs[:9]
    refs = refs[9:]
    if emit_rows:
        wk_ref, wv_ref, gk_ref = refs[:3]
        refs = refs[3:]
    q_ref, kt_ref, vt_ref = refs[:3]
    refs = refs[3:]
    u = _rms(h_ref[...], g_ref[...]).astype(BF16)
    gmat = gmat_ref[...]
    gmat_t = gmat_t_ref[...]
    q = _head_norm_rows(_dot(u, wq_ref[...]), gq_ref[...], gmat, gmat_t)
    q_ref[...] = (q * SB_SCALE).astype(q_ref.dtype)
    kt = _head_norm_cols(_dot_nt(wkt_ref[...], u), gkc_ref[...])
    vt = _dot_nt(wvt_ref[...], u)
    kt_ref[...] = kt
    vt_ref[...] = vt
    if emit_bf16:
        ktb_ref, vtb_ref = refs[:2]
        refs = refs[2:]
        ktb_ref[...] = kt.astype(BF16)
        vtb_ref[...] = vt.astype(BF16)
    if emit_rows:
        k_ref, v_ref = refs
        k_ref[...] = _head_norm_rows(_dot(u, wk_ref[...]), gk_ref[...], gmat, gmat_t)
        v_ref[...] = _dot(u, wv_ref[...])


def _qkv(h, g, aw, nb, t, tm, q_dtype, emit_rows, emit_bf16):
    n = h.shape[0]
    nt = t // tm
    wq, wkt, wvt, wk, wv, gq, gk, gkc, gmat, gmat_t = aw
    row = lambda i: (0, 0)
    tok = pl.BlockSpec((tm, D_MODEL), lambda i: (i, 0))
    wspec = pl.BlockSpec((D_MODEL, D_MODEL), row)
    gspec = pl.BlockSpec((1, D_MODEL), row)
    tr = pl.BlockSpec((None, D_MODEL, tm), lambda i: (i // nt, 0, i % nt))
    in_specs = [tok, gspec, wspec, wspec, wspec, gspec, pl.BlockSpec((D_MODEL, 1), row),
                pl.BlockSpec((D_MODEL, LANES), row), pl.BlockSpec((LANES, D_MODEL), row)]
    args = [h, g, wq, wkt, wvt, gq, gkc, gmat, gmat_t]
    if emit_rows:
        in_specs += [wspec, wspec, gspec]
        args += [wk, wv, gk]
    out_specs = [tok, tr, tr]
    out_shape = [jax.ShapeDtypeStruct((n, D_MODEL), q_dtype),
                 jax.ShapeDtypeStruct((nb, D_MODEL, t), F32),
                 jax.ShapeDtypeStruct((nb, D_MODEL, t), F32)]
    if emit_bf16:
        trb = pl.BlockSpec((None, None, D_MODEL, tm), lambda i: (i // nt, i % nt, 0, 0))
        out_specs += [trb, trb]
        out_shape += [jax.ShapeDtypeStruct((nb, nt, D_MODEL, tm), BF16)] * 2
    if emit_rows:
        out_specs += [tok, tok]
        out_shape += [jax.ShapeDtypeStruct((n, D_MODEL), F32)] * 2
    return pl.pallas_call(
        functools.partial(_qkv_kernel, emit_rows=emit_rows, emit_bf16=emit_bf16),
        grid=(n // tm,),
        in_specs=in_specs,
        out_specs=out_specs,
        out_shape=out_shape,
        compiler_params=_cparams("parallel"),
        name="qkv_proj",
    )(*args)


def _sb_weights(z, carry, tri, valid):
    soft = jnp.log(1.0 + jnp.exp(-jnp.abs(z)))
    log_beta = jnp.minimum(z, 0.0) - soft
    log_keep = -jnp.maximum(z, 0.0) - soft
    if valid is not None:
        log_keep = jnp.where(valid, log_keep, 0.0)
    hi, lo = _split_bf16(log_keep)
    after = _dot(hi, tri) + _dot(lo, tri)
    a = jnp.exp(log_beta + after + carry)
    if valid is not None:
        a = jnp.where(valid, a, 0.0)
    carry = carry + after[:, :1] + log_keep[:, :1]
    return a.astype(BF16), carry


def _attn_prompt_kernel(bias_ref, q_ref, kt_ref, vt_ref, tri_ref, o_ref):
    tq = ATTN_BLOCK
    hp = pl.program_id(1)
    qi = pl.program_id(2)
    q2 = q_ref[...]
    tri = tri_ref[...]
    lane = lax.broadcasted_iota(jnp.int32, (1, LANES), 1)
    sub = lax.broadcasted_iota(jnp.int32, (LANES, 1), 0)
    qh = [jnp.where(m, q2, jnp.zeros_like(q2)) for m in (lane < HEAD_DIM, lane >= HEAD_DIM)]
    row_mask = [sub < HEAD_DIM, sub >= HEAD_DIM]
    bias = [bias_ref[2 * hp], bias_ref[2 * hp + 1]]
    row = lax.broadcasted_iota(jnp.int32, (tq, tq), 0)
    col = lax.broadcasted_iota(jnp.int32, (tq, tq), 1)
    causal = col < row

    def tile(j, carries, valid):
        kblk = kt_ref[j]
        vblk = vt_ref[j]
        out = jnp.zeros((tq, LANES), F32)
        new = []
        for hh in range(2):
            z = _dot(qh[hh], kblk) + bias[hh]
            a, c = _sb_weights(z, carries[hh], tri, valid)
            vm = jnp.where(row_mask[hh], vblk, jnp.zeros_like(vblk))
            out = out + _dot_nt(a, vm)
            new.append(c)
        return out, tuple(new)

    zero = jnp.zeros((tq, 1), F32)
    acc, carries = tile(qi, (zero, zero), causal)

    def body(j, st):
        acc, carries = st
        out, carries = tile(qi - 1 - j, carries, None)
        return acc + out, carries

    acc, _ = lax.fori_loop(0, qi, body, (acc, carries))
    o_ref[...] = acc.astype(o_ref.dtype)


def _attn_prompt(q, ktb, vtb, bias, tri):
    b, t, _ = q.shape
    tq = ATTN_BLOCK
    nt = t // tq
    kv = pl.BlockSpec((None, nt, LANES, tq), lambda bi, hp, qi: (bi, 0, hp, 0))
    qo = pl.BlockSpec((None, tq, LANES), lambda bi, hp, qi: (bi, qi, hp))
    return pl.pallas_call(
        _attn_prompt_kernel,
        grid=(b, N_HEADS // 2, nt),
        in_specs=[
            pl.BlockSpec(memory_space=pltpu.SMEM),
            qo, kv, kv,
            pl.BlockSpec((tq, tq), lambda bi, hp, qi: (0, 0)),
        ],
        out_specs=qo,
        out_shape=jax.ShapeDtypeStruct((b, t, D_MODEL), BF16),
        compiler_params=_cparams("parallel", "parallel", "arbitrary"),
        name="attn_prompt",
    )(bias, q, ktb, vtb, tri)


def _attn_sample_kernel(pt_ref, q_ref, kn_ref, vn_ref, kp_ref, vp_ref, bd_ref, bias_ref, tri_ref,
                        o_ref, qbd_ref, acc_ref, carry_ref, *, n_pages, dec_seq):
    p = pl.program_id(1)
    page = kp_ref.shape[1]
    rows = N_HEADS * dec_seq

    @pl.when(p == 0)
    def _():
        q = q_ref[...]
        qbd = (jnp.concatenate([q] * N_HEADS, axis=0) * bd_ref[...]).astype(BF16)
        qbd_ref[...] = qbd
        pad = jnp.zeros((page - dec_seq, D_MODEL), F32)
        kn = jnp.concatenate([kn_ref[...], pad], axis=0).astype(BF16)
        vn = jnp.concatenate([vn_ref[...], pad], axis=0).astype(BF16)
        r = lax.broadcasted_iota(jnp.int32, (rows, page), 0)
        s = lax.broadcasted_iota(jnp.int32, (rows, page), 1)
        z = _dot_nt(qbd, kn) + bias_ref[...]
        a, carry = _sb_weights(z, jnp.zeros((rows, 1), F32), tri_ref[...], s < r % dec_seq)
        acc_ref[...] = _dot(a, vn)
        carry_ref[...] = carry

    @pl.when(p > 0)
    def _():
        z = _dot(qbd_ref[...], kp_ref[...].astype(BF16)) + bias_ref[...]
        a, carry = _sb_weights(z, carry_ref[...], tri_ref[...], None)
        acc_ref[...] += _dot_nt(a, vp_ref[...].astype(BF16))
        carry_ref[...] = carry

    @pl.when(p == n_pages)
    def _():
        own = (acc_ref[...] * bd_ref[...]).reshape(N_HEADS, dec_seq, D_MODEL)
        o_ref[...] = jnp.sum(own, axis=0)


def _attn_sample(q, kn, vn, cache_kt, cache_vt, layer, page_table, bd, bias_rows, tri):
    b, t, _ = q.shape
    n_pages = page_table.shape[1]
    page = cache_kt.shape[3]
    rows = N_HEADS * t
    kern = functools.partial(_attn_sample_kernel, n_pages=n_pages, dec_seq=t)

    def page_map(bi, p, pt):
        return (layer, pt[bi * n_pages + n_pages - jnp.maximum(p, 1)], 0, 0)

    tok = pl.BlockSpec((None, t, D_MODEL), lambda bi, p, pt: (bi, 0, 0))
    const = lambda bi, p, pt: (0, 0)
    grid_spec = pltpu.PrefetchScalarGridSpec(
        num_scalar_prefetch=1,
        grid=(b, n_pages + 1),
        in_specs=[
            tok, tok, tok,
            pl.BlockSpec((None, None, D_MODEL, page), page_map),
            pl.BlockSpec((None, None, D_MODEL, page), page_map),
            pl.BlockSpec((rows, D_MODEL), const),
            pl.BlockSpec((rows, 1), const),
            pl.BlockSpec((page, page), const),
        ],
        out_specs=tok,
        scratch_shapes=[
            pltpu.VMEM((rows, D_MODEL), BF16),
            pltpu.VMEM((rows, D_MODEL), F32),
            pltpu.VMEM((rows, 1), F32),
        ],
    )
    return pl.pallas_call(
        kern,
        grid_spec=grid_spec,
        out_shape=jax.ShapeDtypeStruct((b, t, D_MODEL), F32),
        compiler_params=_cparams("parallel", "arbitrary"),
        name="attn_sample",
    )(page_table.reshape(-1), q, kn, vn, cache_kt, cache_vt, bd, bias_rows, tri)


def _proj_res_kernel(x_ref, w_ref, r_ref, o_ref):
    o_ref[...] = r_ref[...] + _dot(x_ref[...].astype(BF16), w_ref[...])


def _proj_res(x, w, res, tm):
    n, kdim = x.shape
    tok = pl.BlockSpec((tm, D_MODEL), lambda i: (i, 0))
    return pl.pallas_call(
        _proj_res_kernel,
        grid=(n // tm,),
        in_specs=[
            pl.BlockSpec((tm, kdim), lambda i: (i, 0)),
            pl.BlockSpec((kdim, D_MODEL), lambda i: (0, 0)),
            tok,
        ],
        out_specs=tok,
        out_shape=jax.ShapeDtypeStruct((n, D_MODEL), F32),
        compiler_params=_cparams("parallel"),
        name="proj_res",
    )(x, w, res)


def _route(logits):
    lane = lax.broadcasted_iota(jnp.int32, logits.shape, 1)
    neg = jnp.float32(-jnp.inf)
    is_group = lane < N_GROUPS
    gl = jnp.where(is_group, logits, neg)
    gmax = jnp.max(gl, axis=-1, keepdims=True)
    gi = jnp.min(jnp.where(gl == gmax, lane, ROUTER_LANES), axis=-1, keepdims=True)
    gsum = jnp.sum(jnp.where(is_group, jnp.exp(logits - gmax), 0.0), axis=-1, keepdims=True)
    g_prob = 1.0 / gsum
    e_idx = lane - N_GROUPS
    in_group = jnp.logical_and(jnp.logical_and(e_idx >= 0, e_idx < N_EXPERTS),
                               lax.shift_right_arithmetic(e_idx, EXPERTS_PER_GROUP_LOG2) == gi)
    el = jnp.where(in_group, logits, neg)
    v1 = jnp.max(el, axis=-1, keepdims=True)
    i1 = jnp.min(jnp.where(el == v1, lane, ROUTER_LANES), axis=-1, keepdims=True)
    el2 = jnp.where(lane == i1, neg, el)
    v2 = jnp.max(el2, axis=-1, keepdims=True)
    i2 = jnp.min(jnp.where(el2 == v2, lane, ROUTER_LANES), axis=-1, keepdims=True)
    e21 = jnp.exp(v2 - v1)
    w1 = g_prob / (1.0 + e21)
    w2 = g_prob * e21 / (1.0 + e21)
    return jnp.where(lane == i1, w1, 0.0) + jnp.where(lane == i2, w2, 0.0)


def _moe_kernel(h_ref, g_ref, wr_hi_ref, wr_lo_ref, wg_ref, wu_ref, wd_ref, o_ref,
                u_ref, comb_ref, acc_ref):
    e = pl.program_id(1)

    @pl.when(e == 0)
    def _():
        u = _rms(h_ref[...], g_ref[...])
        u_hi, u_lo = _split_bf16(u)
        logits = (_dot(u_hi, wr_hi_ref[...]) + _dot(u_lo, wr_hi_ref[...])
                  + _dot(u_hi, wr_lo_ref[...]))
        u_ref[...] = u_hi
        comb_ref[...] = _route(logits)
        acc_ref[...] = jnp.zeros_like(acc_ref)

    u = u_ref[...]
    hg = _dot(u, wg_ref[...])
    hu = _dot(u, wu_ref[...])
    lane = lax.broadcasted_iota(jnp.int32, comb_ref.shape, 1)
    cw = jnp.sum(jnp.where(lane == e + N_GROUPS, comb_ref[...], 0.0), axis=-1, keepdims=True)
    act = (hg * _sigmoid(hg)) * hu * cw
    acc_ref[...] += _dot(act.astype(BF16), wd_ref[...])

    @pl.when(e == N_EXPERTS - 1)
    def _():
        o_ref[...] = h_ref[...] + acc_ref[...]


def _moe(h, g, wr_hi, wr_lo, wg, wu, wd, layer, tm):
    n = h.shape[0]
    tok = pl.BlockSpec((tm, D_MODEL), lambda i, e: (i, 0))
    const = lambda i, e: (0, 0)
    return pl.pallas_call(
        _moe_kernel,
        grid=(n // tm, N_EXPERTS),
        in_specs=[
            tok,
            pl.BlockSpec((1, D_MODEL), const),
            pl.BlockSpec((D_MODEL, ROUTER_LANES), const),
            pl.BlockSpec((D_MODEL, ROUTER_LANES), const),
            pl.BlockSpec((None, None, D_MODEL, D_EXPERT), lambda i, e: (layer, e, 0, 0)),
            pl.BlockSpec((None, None, D_MODEL, D_EXPERT), lambda i, e: (layer, e, 0, 0)),
            pl.BlockSpec((None, None, D_EXPERT, D_MODEL), lambda i, e: (layer, e, 0, 0)),
        ],
        out_specs=tok,
        out_shape=jax.ShapeDtypeStruct((n, D_MODEL), F32),
        scratch_shapes=[
            pltpu.VMEM((tm, D_MODEL), BF16),
            pltpu.VMEM((tm, ROUTER_LANES), F32),
            pltpu.VMEM((tm, D_MODEL), F32),
        ],
        compiler_params=_cparams("parallel", "arbitrary"),
        name="moe",
    )(h, g, wr_hi, wr_lo, wg, wu, wd)


def _ple_kernel(h_ref, p_ref, g_ref, wp_ref, wpg_ref, o_ref):
    h = h_ref[...]
    gate = _sigmoid(_dot(_rms(h, g_ref[...]).astype(BF16), wpg_ref[...]))
    o_ref[...] = h + _dot(p_ref[...].astype(BF16), wp_ref[...]) * gate


def _ple(h, p, g, wp, wpg, layer, tm):
    n = h.shape[0]
    tok = pl.BlockSpec((tm, D_MODEL), lambda i: (i, 0))
    return pl.pallas_call(
        _ple_kernel,
        grid=(n // tm,),
        in_specs=[
            tok,
            pl.BlockSpec((None, tm, D_PLE), lambda i: (layer, i, 0)),
            pl.BlockSpec((1, D_MODEL), lambda i: (0, 0)),
            pl.BlockSpec((D_PLE, D_MODEL), lambda i: (0, 0)),
            pl.BlockSpec((D_MODEL, D_MODEL), lambda i: (0, 0)),
        ],
        out_specs=tok,
        out_shape=jax.ShapeDtypeStruct((n, D_MODEL), F32),
        compiler_params=_cparams("parallel"),
        name="ple",
    )(h, p, g, wp, wpg)


def _strict_lower(n):
    j = lax.broadcasted_iota(jnp.int32, (n, n), 0)
    s = lax.broadcasted_iota(jnp.int32, (n, n), 1)
    return (j > s).astype(BF16)


def _steps_major(x):
    return jnp.swapaxes(x, 0, 1)


def kernel(x_prompt, x_sample, state_conv, cache_k, cache_v, page_table, p_prompt, p_sample,
           g_mix, g_ffn, g_ple, conv_w1, conv_b1, conv_wdw, conv_bdw, conv_gln, conv_bln,
           conv_w2, conv_b2, attn_wqkv, attn_gq, attn_gk, attn_bias, attn_wo,
           moe_wrg, moe_wre, moe_wg, moe_wu, moe_wd, ple_w, ple_wg):
    depth = g_mix.shape[0]
    batch, seq, _ = x_prompt.shape
    dec_batch, dec_seq, _ = x_sample.shape
    n_p = batch * seq
    n_s = dec_batch * dec_seq
    page = cache_k.shape[2]

    hp = x_prompt.reshape(n_p, D_MODEL)
    hs = _steps_major(x_sample).reshape(n_s, D_MODEL)
    pp = p_prompt.reshape(depth, n_p, D_PLE)
    ps = jnp.swapaxes(p_sample, 1, 2).reshape(depth, n_s, D_PLE)

    wg_b = moe_wg.astype(BF16)
    wu_b = moe_wu.astype(BF16)
    wd_b = moe_wd.astype(BF16)
    ckt = cache_k.transpose(0, 1, 3, 4, 2).reshape(cache_k.shape[0], cache_k.shape[1], D_MODEL, page)
    cvt = cache_v.transpose(0, 1, 3, 4, 2).reshape(cache_v.shape[0], cache_v.shape[1], D_MODEL, page)

    chan = jnp.arange(D_MODEL) // HEAD_DIM
    gmat = ((chan[:, None] == jnp.arange(LANES)[None, :]).astype(F32) / HEAD_DIM).astype(BF16)
    gmat_t = (jnp.arange(LANES)[:, None] == chan[None, :]).astype(BF16)
    bd = (jnp.arange(N_HEADS * dec_seq)[:, None] // dec_seq == chan[None, :]).astype(F32)
    tri_p = _strict_lower(ATTN_BLOCK)
    tri_s = _strict_lower(page)

    conv_p, conv_s, k_p, v_p, k_s, v_s = [], [], [], [], [], []
    for i in range(depth):
        gm = g_mix[i].reshape(1, D_MODEL)
        if i % 2 == 0:
            c = i // 2
            w1 = conv_w1[c].astype(BF16)
            b1 = conv_b1[c].reshape(1, 2 * D_MODEL)
            cw = (conv_wdw[c], conv_bdw[c].reshape(1, D_MODEL), conv_gln[c].reshape(1, D_MODEL),
                  conv_bln[c].reshape(1, D_MODEL), conv_w2[c].astype(BF16),
                  conv_b2[c].reshape(1, D_MODEL))
            gp = _glu(hp, gm, w1, b1, 512).reshape(batch, seq, D_MODEL)
            hp = _conv_prompt(gp, hp.reshape(batch, seq, D_MODEL), cw, 256).reshape(n_p, D_MODEL)
            conv_p.append(gp[:, seq - (CONV_WIDTH - 1):])

            gs = _glu(hs, gm, w1, b1, 512).reshape(dec_seq, dec_batch, D_MODEL)
            hs3, nst = _conv_sample(_steps_major(state_conv[c]), gs,
                                    hs.reshape(dec_seq, dec_batch, D_MODEL), cw, 16)
            hs = hs3.reshape(n_s, D_MODEL)
            conv_s.append(_steps_major(nst))
        else:
            a = i // 2
            wqkv = attn_wqkv[a].astype(BF16)
            wk = wqkv[:, D_MODEL:2 * D_MODEL]
            wv = wqkv[:, 2 * D_MODEL:]
            gk = attn_gk[a]
            aw = (wqkv[:, :D_MODEL], wk.T, wv.T, wk, wv,
                  jnp.tile(attn_gq[a], N_HEADS).reshape(1, D_MODEL),
                  jnp.tile(gk, N_HEADS).reshape(1, D_MODEL),
                  jnp.tile(gk, N_HEADS).reshape(D_MODEL, 1), gmat, gmat_t)
            wo = attn_wo[a].astype(BF16)
            bias = attn_bias[a]
            bias_rows = jnp.repeat(bias, dec_seq).reshape(N_HEADS * dec_seq, 1)

            q, kt, vt, ktb, vtb = _qkv(hp, gm, aw, batch, seq, ATTN_BLOCK, BF16, False, True)
            o = _attn_prompt(q.reshape(batch, seq, D_MODEL), ktb, vtb, bias, tri_p)
            hp = _proj_res(o.reshape(n_p, D_MODEL), wo, hp, 512)
            to_out = lambda x: x.reshape(batch, N_HEADS, HEAD_DIM, seq).transpose(0, 3, 1, 2)
            k_p.append(to_out(kt))
            v_p.append(to_out(vt))

            q, kt, vt, k, v = _qkv(hs, gm, aw, dec_seq, dec_batch, LANES, F32, True, False)
            bt = lambda x: _steps_major(x.reshape(dec_seq, dec_batch, D_MODEL))
            o = _attn_sample(bt(q), bt(k), bt(v), ckt, cvt, a, page_table, bd, bias_rows, tri_s)
            hs = _proj_res(_steps_major(o).reshape(n_s, D_MODEL), wo, hs, 512)
            to_out = lambda x: x.reshape(dec_seq, N_HEADS, HEAD_DIM, dec_batch).transpose(3, 0, 1, 2)
            k_s.append(to_out(kt))
            v_s.append(to_out(vt))

        wr = jnp.concatenate([moe_wrg[i], moe_wre[i]], axis=1)
        wr = jnp.pad(wr, ((0, 0), (0, ROUTER_LANES - wr.shape[1])))
        wr_hi = wr.astype(BF16)
        wr_lo = (wr - wr_hi.astype(F32)).astype(BF16)
        gf = g_ffn[i].reshape(1, D_MODEL)
        hp = _moe(hp, gf, wr_hi, wr_lo, wg_b, wu_b, wd_b, i, 512)
        hs = _moe(hs, gf, wr_hi, wr_lo, wg_b, wu_b, wd_b, i, 512)

        gpl = g_ple[i].reshape(1, D_MODEL)
        wp = ple_w[i].astype(BF16)
        wpg = ple_wg[i].astype(BF16)
        hp = _ple(hp, pp, gpl, wp, wpg, i, 512)
        hs = _ple(hs, ps, gpl, wp, wpg, i, 512)

    y_sample = _steps_major(hs.reshape(dec_seq, dec_batch, D_MODEL))
    return (hp.reshape(batch, seq, D_MODEL), y_sample,
            jnp.stack(conv_p), jnp.stack(conv_s), jnp.stack(k_p), jnp.stack(v_p),
            jnp.stack(k_s), jnp.stack(v_s))
```
